```python
import math
import jax, jax.numpy as jnp
from jax import lax
import numpy as np

D_MODEL = 2048
BATCH = 8
SEQ = 4096
DEPTH = 4

N_MIXERS = 3
N_RET_LAYERS = len(range(0, DEPTH, N_MIXERS))
N_CONV_LAYERS = len(range(1, DEPTH, N_MIXERS))
N_DIFF_LAYERS = len(range(2, DEPTH, N_MIXERS))

RET_HEADS = 8
RET_DK = D_MODEL // RET_HEADS
RET_DV = 2 * RET_DK
RET_CHUNK = 128

CONV_WIDTH = 31

DIFF_HEADS = 8
DIFF_DH = D_MODEL // DIFF_HEADS // 2
Q_BLOCK = 128

D_FF = 5632
FFN_CONV_WIDTH = 3

ROPE_THETA = 10000.0
EPS = 1e-6
NEG_INF = -1e30

kernel_name = "hybrid_retention_conformer_diffattn_trunk"


def rms_norm(x, g):
    xf = x.astype(jnp.float32)
    y = xf * lax.rsqrt(jnp.mean(xf * xf, axis=-1, keepdims=True) + EPS)
    return (y * g.astype(jnp.float32)).astype(x.dtype)


def rotary(t, positions):
    d = t.shape[-1]
    inv_freq = ROPE_THETA ** (-jnp.arange(0, d, 2, dtype=jnp.float32) / d)
    ang = positions.astype(jnp.float32)[..., None] * inv_freq
    ang = ang.reshape(ang.shape[:2] + (1,) * (t.ndim - 3) + (d // 2,))
    cos, sin = jnp.cos(ang), jnp.sin(ang)
    tf = t.astype(jnp.float32)
    t1, t2 = tf[..., : d // 2], tf[..., d // 2:]
    return jnp.concatenate([t1 * cos - t2 * sin, t2 * cos + t1 * sin], axis=-1).astype(t.dtype)


def causal_dwconv(x, w):
    width, ch = w.shape
    return lax.conv_general_dilated(
        x, w[:, None, :].astype(x.dtype), window_strides=(1,), padding=[(width - 1, 0)],
        dimension_numbers=('NWC', 'WIO', 'NWC'), feature_group_count=ch)


def ada_modulation(c, w, b):
    m = jax.nn.silu(c) @ w + b
    shift, scale, gate = jnp.split(m[:, None, :], 3, axis=-1)
    return shift, scale, gate


def retention(h, positions, w_in, w_out):
    B, S, _ = h.shape
    H, dk, dv = RET_HEADS, RET_DK, RET_DV
    f32 = jnp.float32
    q, k, v, g = jnp.split(h @ w_in, [H * dk, 2 * H * dk, 2 * H * dk + H * dv], axis=-1)
    q = rotary(q.reshape(B, S, H, dk), positions).astype(f32)
    k = rotary(k.reshape(B, S, H, dk), positions).astype(f32) * (dk ** -0.5)
    v = v.reshape(B, S, H, dv).astype(f32)
    log_gamma = jnp.log1p(-jnp.exp2(-5.0 - jnp.arange(H, dtype=f32)))
    idx = jnp.arange(RET_CHUNK, dtype=f32)
    rel = idx[:, None] - idx[None, :]
    decay = jnp.where(rel >= 0, jnp.exp(jnp.maximum(rel, 0.0)[None] * log_gamma[:, None, None]), 0.0)
    cross_decay = jnp.exp((idx + 1.0)[None] * log_gamma[:, None])
    state_decay = jnp.exp((RET_CHUNK - 1.0 - idx)[None] * log_gamma[:, None])
    chunk_decay = jnp.exp(RET_CHUNK * log_gamma)
    n_chunks = S // RET_CHUNK

    def to_chunks(t):
        return t.reshape(B, n_chunks, RET_CHUNK, H, t.shape[-1]).transpose(1, 0, 3, 2, 4)

    def step(state, qkv):
        qc, kc, vc = qkv
        scores = jnp.einsum('bhqd,bhkd->bhqk', qc, kc) * decay
        inner = jnp.einsum('bhqk,bhkv->bhqv', scores, vc)
        cross = jnp.einsum('bhqd,bhdv->bhqv', qc, state) * cross_decay[:, :, None]
        new_state = state * chunk_decay[:, None, None] + jnp.einsum(
            'bhkd,bhkv->bhdv', kc * state_decay[:, :, None], vc)
        return new_state, inner + cross

    state0 = jnp.zeros((B, H, dk, dv), f32)
    _, o = lax.scan(step, state0, (to_chunks(q), to_chunks(k), to_chunks(v)))
    o = o.transpose(1, 0, 3, 2, 4).reshape(B, S, H, dv)
    mu = jnp.mean(o, axis=-1, keepdims=True)
    var = jnp.mean(jnp.square(o - mu), axis=-1, keepdims=True)
    o = ((o - mu) * lax.rsqrt(var + EPS)).reshape(B, S, H * dv).astype(h.dtype)
    return (jax.nn.silu(g) * o) @ w_out


def conformer_conv(h, w_pw1, b_pw1, w_dw, b_dw, ln_g, ln_b, w_pw2, b_pw2):
    a, b = jnp.split(h @ w_pw1 + b_pw1, 2, axis=-1)
    u = a * jax.nn.sigmoid(b)
    u = causal_dwconv(u, w_dw) + b_dw
    uf = u.astype(jnp.float32)
    mu = jnp.mean(uf, axis=-1, keepdims=True)
    var = jnp.mean(jnp.square(uf - mu), axis=-1, keepdims=True)
    un = (uf - mu) * lax.rsqrt(var + EPS) * ln_g.astype(jnp.float32) + ln_b.astype(jnp.float32)
    u = jax.nn.silu(un).astype(h.dtype)
    return u @ w_pw2 + b_pw2


def diff_attention(h, positions, w_in, lam_params, subln_g, w_out, layer_idx):
    B, S, D = h.shape
    H, dh = DIFF_HEADS, DIFF_DH
    f32 = jnp.float32
    q, k, v = jnp.split(h @ w_in, 3, axis=-1)
    q = rotary(q.reshape(B, S, H, 2, dh), positions)
    k = rotary(k.reshape(B, S, H, 2, dh), positions)
    v = v.reshape(B, S, H, 2 * dh).astype(f32)
    lambda_init = 0.8 - 0.6 * math.exp(-0.3 * layer_idx)
    lp = lam_params.astype(f32)
    lam = jnp.exp(jnp.sum(lp[0] * lp[1])) - jnp.exp(jnp.sum(lp[2] * lp[3])) + lambda_init
    n_blocks = S // Q_BLOCK
    q_blocks = q.reshape(B, n_blocks, Q_BLOCK, H, 2, dh).transpose(1, 0, 2, 3, 4, 5)
    key_pos = jnp.arange(S)
    scale = dh ** -0.5

    def attend(args):
        q_blk, blk = args
        s = jnp.einsum('bqhcd,bkhcd->bhcqk', q_blk, k).astype(f32) * scale
        q_pos = blk * Q_BLOCK + jnp.arange(Q_BLOCK)
        s = jnp.where(key_pos[None, :] <= q_pos[:, None], s, NEG_INF)
        p = jax.nn.softmax(s, axis=-1)
        w = p[:, :, 0] - lam * p[:, :, 1]
        return jnp.einsum('bhqk,bkhv->bqhv', w, v)

    o = lax.map(attend, (q_blocks, jnp.arange(n_blocks)))
    o = o.transpose(1, 0, 2, 3, 4).reshape(B, S, H, 2 * dh)
    o = rms_norm(o, subln_g) * (1.0 - lambda_init)
    return o.reshape(B, S, D).astype(h.dtype) @ w_out


def conv_ffn(h, w_in, w_dw, w_down):
    gate, up = jnp.split(h @ w_in, 2, axis=-1)
    gate = causal_dwconv(gate, w_dw)
    return (jax.nn.silu(gate) * up) @ w_down


def setup_inputs(seed: int = 0) -> dict:
    key = jax.random.key(seed)
    ks = jax.random.split(key, 24)
    f32 = jnp.float32
    D, F = D_MODEL, D_FF
    nrm = lambda k, shape, s: jax.random.normal(k, shape, f32) * s
    ret_in_w = 2 * RET_HEADS * RET_DK + 2 * RET_HEADS * RET_DV
    return {
        'x': jax.random.normal(ks[0], (BATCH, SEQ, D), f32),
        'c': jax.random.normal(ks[1], (BATCH, D), f32),
        'positions': jnp.broadcast_to(jnp.arange(SEQ, dtype=jnp.int32)[None, :], (BATCH, SEQ)),
        'mod_w': nrm(ks[2], (DEPTH, 2, D, 3 * D), 0.5 * D ** -0.5),
        'mod_b': nrm(ks[3], (DEPTH, 2, 3 * D), 0.02),
        'norm_g': 1.0 + nrm(ks[4], (DEPTH, 4, D), 0.02),
        'ret_w_in': nrm(ks[5], (N_RET_LAYERS, D, ret_in_w), D ** -0.5),
        'ret_w_out': nrm(ks[6], (N_RET_LAYERS, RET_HEADS * RET_DV, D), (RET_HEADS * RET_DV) ** -0.5),
        'conv_w_pw1': nrm(ks[7], (N_CONV_LAYERS, D, 2 * D), D ** -0.5),
        'conv_b_pw1': nrm(ks[8], (N_CONV_LAYERS, 2 * D), 0.02),
        'conv_w_dw': nrm(ks[9], (N_CONV_LAYERS, CONV_WIDTH, D), CONV_WIDTH ** -0.5),
        'conv_b_dw': nrm(ks[10], (N_CONV_LAYERS, D), 0.02),
        'conv_ln_g': 1.0 + nrm(ks[11], (N_CONV_LAYERS, D), 0.02),
        'conv_ln_b': nrm(ks[12], (N_CONV_LAYERS, D), 0.02),
        'conv_w_pw2': nrm(ks[13], (N_CONV_LAYERS, D, D), D ** -0.5),
        'conv_b_pw2': nrm(ks[14], (N_CONV_LAYERS, D), 0.02),
        'diff_w_in': nrm(ks[15], (N_DIFF_LAYERS, D, 3 * D), D ** -0.5),
        'diff_lambda': nrm(ks[16], (N_DIFF_LAYERS, 4, DIFF_DH), 0.1),
        'diff_subln_g': 1.0 + nrm(ks[17], (N_DIFF_LAYERS, 2 * DIFF_DH), 0.02),
        'diff_w_out': nrm(ks[18], (N_DIFF_LAYERS, D, D), D ** -0.5),
        'ffn_w_in': nrm(ks[19], (DEPTH, D, 2 * F), D ** -0.5),
        'ffn_w_dw': nrm(ks[20], (DEPTH, FFN_CONV_WIDTH, F), FFN_CONV_WIDTH ** -0.5),
        'ffn_w_down': nrm(ks[21], (DEPTH, F, D), F ** -0.5),
    }


def reference(x, c, positions, mod_w, mod_b, norm_g, ret_w_in, ret_w_out,
              conv_w_pw1, conv_b_pw1, conv_w_dw, conv_b_dw, conv_ln_g, conv_ln_b,
              conv_w_pw2, conv_b_pw2, diff_w_in, diff_lambda, diff_subln_g, diff_w_out,
              ffn_w_in, ffn_w_dw, ffn_w_down):
    for i in range(DEPTH):
        kind, slot = i % N_MIXERS, i // N_MIXERS
        shift, scale, gate = ada_modulation(c, mod_w[i, 0], mod_b[i, 0])
        h = rms_norm(x, norm_g[i, 0]) * (1.0 + scale) + shift
        if kind == 0:
            y = retention(h, positions, ret_w_in[slot], ret_w_out[slot])
        elif kind == 1:
            y = conformer_conv(h, conv_w_pw1[slot], conv_b_pw1[slot], conv_w_dw[slot], conv_b_dw[slot],
                               conv_ln_g[slot], conv_ln_b[slot], conv_w_pw2[slot], conv_b_pw2[slot])
        else:
            y = diff_attention(h, positions, diff_w_in[slot], diff_lambda[slot], diff_subln_g[slot],
                               diff_w_out[slot], i)
        x = x + gate * rms_norm(y, norm_g[i, 1])
        shift, scale, gate = ada_modulation(c, mod_w[i, 1], mod_b[i, 1])
        h = rms_norm(x, norm_g[i, 2]) * (1.0 + scale) + shift
        y = conv_ffn(h, ffn_w_in[i], ffn_w_dw[i], ffn_w_down[i])
        x = x + gate * rms_norm(y, norm_g[i, 3])
    return x
```

```python
import functools
import math

import jax
import jax.numpy as jnp
from jax import lax
from jax.experimental import pallas as pl
from jax.experimental.pallas import tpu as pltpu

EPS = 1e-6
ROPE_THETA = 10000.0
NEG_INF = -1e30

RET_DK = 256
RET_DV = 512
DIFF_DH = 128
LANES = 128
CONV_HALO = 32

VMEM_LIMIT = 48 * 1024 * 1024

F32 = jnp.float32
BF16 = jnp.bfloat16


def _params(n_axes):
    return pltpu.CompilerParams(dimension_semantics=("arbitrary",) * n_axes,
                                vmem_limit_bytes=VMEM_LIMIT)


def _pick(total, target):
    if total <= target:
        return total
    t = target - target % LANES
    while t > LANES and total % t:
        t -= LANES
    assert total % t == 0, (total, target)
    return t


def _sigmoid(v):
    return 1.0 / (1.0 + jnp.exp(-v))


def _silu(v):
    return v * _sigmoid(v)


def _mod_kernel(c_ref, w_ref, b_ref, o_ref):
    s = _silu(c_ref[...]).astype(BF16)
    o_ref[0] = jnp.dot(s, w_ref[0].astype(BF16), preferred_element_type=F32) + b_ref[0]


def _modulation(c, mod_w, mod_b):
    depth, two, d, d3 = mod_w.shape
    n_sub = depth * two
    b = c.shape[0]
    bn = _pick(d3, 1536)
    w = mod_w.reshape(n_sub, d, d3)
    bias = mod_b.reshape(n_sub, 1, d3)
    return pl.pallas_call(
        _mod_kernel,
        out_shape=jax.ShapeDtypeStruct((n_sub, b, d3), F32),
        grid=(n_sub, d3 // bn),
        in_specs=[pl.BlockSpec((b, d), lambda l, n: (0, 0)),
                  pl.BlockSpec((1, d, bn), lambda l, n: (l, 0, n)),
                  pl.BlockSpec((1, 1, bn), lambda l, n: (l, 0, n))],
        out_specs=pl.BlockSpec((1, b, bn), lambda l, n: (l, 0, n)),
        compiler_params=_params(2),
        name="modulation",
    )(c, w, bias)


def _modulated_norm(x, mod_ref, g_ref):
    ms = jnp.mean(x * x, axis=-1, keepdims=True)
    y = x * lax.rsqrt(ms + EPS) * g_ref[...]
    return y * (1.0 + mod_ref[0, 1:2, :]) + mod_ref[0, 0:1, :]


def _in_rot_kernel(x_ref, mod_ref, g_ref, w_ref, cs_ref, t1_ref, t2_ref, o_ref, h_ref, *,
                   n_rot, pair_width):
    n = pl.program_id(1)

    @pl.when(n == 0)
    def _():
        h_ref[...] = _modulated_norm(x_ref[...], mod_ref, g_ref).astype(BF16)

    acc = jnp.dot(h_ref[...], w_ref[...], preferred_element_type=F32)
    bn = acc.shape[1]

    @pl.when(n < n_rot)
    def _():
        t1 = t1_ref[...]
        t2 = t2_ref[...]
        sc = acc * cs_ref[...]
        if pair_width == 2 * LANES:
            for i in range(bn // (2 * LANES)):
                lo = sc[:, 2 * i * LANES:(2 * i + 1) * LANES]
                hi = sc[:, (2 * i + 1) * LANES:(2 * i + 2) * LANES]
                o_ref[:, 2 * i * LANES:(2 * i + 1) * LANES] = (lo * t1 - hi * t2).astype(o_ref.dtype)
                o_ref[:, (2 * i + 1) * LANES:(2 * i + 2) * LANES] = (hi * t1 + lo * t2).astype(o_ref.dtype)
        else:
            for i in range(bn // LANES):
                t = sc[:, i * LANES:(i + 1) * LANES]
                sw = pltpu.roll(t, LANES // 2, axis=1)
                o_ref[:, i * LANES:(i + 1) * LANES] = (t * t1 + sw * t2).astype(o_ref.dtype)

    @pl.when(n >= n_rot)
    def _():
        o_ref[...] = acc.astype(o_ref.dtype)


def _in_proj_rot(x, mod, g, w, col_scale, t1, t2, *, seq, rot_cols, pair_width):
    m, d = x.shape
    n_out = w.shape[1]
    bm = _pick(seq, 512)
    bn = _pick(math.gcd(rot_cols, n_out), 1024)
    tiles_per_seq = seq // bm
    kern = functools.partial(_in_rot_kernel, n_rot=rot_cols // bn, pair_width=pair_width)
    n_rot = rot_cols // bn
    return pl.pallas_call(
        kern,
        out_shape=jax.ShapeDtypeStruct((m, n_out), BF16),
        grid=(m // bm, n_out // bn),
        in_specs=[pl.BlockSpec((bm, d), lambda i, n: (i, 0)),
                  pl.BlockSpec((1, 3, d), lambda i, n: (i // tiles_per_seq, 0, 0)),
                  pl.BlockSpec((1, d), lambda i, n: (0, 0)),
                  pl.BlockSpec((d, bn), lambda i, n: (0, n)),
                  pl.BlockSpec((1, bn), lambda i, n: (0, jnp.minimum(n, n_rot - 1))),
                  pl.BlockSpec((bm, LANES), lambda i, n: (i, 0)),
                  pl.BlockSpec((bm, LANES), lambda i, n: (i, 0))],
        out_specs=pl.BlockSpec((bm, bn), lambda i, n: (i, n)),
        scratch_shapes=[pltpu.VMEM((bm, d), BF16)],
        compiler_params=_params(2),
        name="in_proj_rot",
    )(x, mod, g, w, col_scale, t1, t2)


def _in_glu_kernel(x_ref, mod_ref, g_ref, wa_ref, wb_ref, ba_ref, bb_ref, o_ref, h_ref):
    n = pl.program_id(1)

    @pl.when(n == 0)
    def _():
        h_ref[...] = _modulated_norm(x_ref[...], mod_ref, g_ref).astype(BF16)

    h = h_ref[...]
    a = jnp.dot(h, wa_ref[...], preferred_element_type=F32) + ba_ref[...]
    b = jnp.dot(h, wb_ref[...], preferred_element_type=F32) + bb_ref[...]
    o_ref[...] = (a * _sigmoid(b)).astype(o_ref.dtype)


def _in_proj_glu(x, mod, g, w, bias, *, seq):
    m, d = x.shape
    half = w.shape[1] // 2
    bm = _pick(seq, 512)
    bn = _pick(half, 512)
    nb = half // bn
    tiles_per_seq = seq // bm
    bias2 = bias.reshape(1, 2 * half)
    return pl.pallas_call(
        _in_glu_kernel,
        out_shape=jax.ShapeDtypeStruct((m, half), F32),
        grid=(m // bm, nb),
        in_specs=[pl.BlockSpec((bm, d), lambda i, n: (i, 0)),
                  pl.BlockSpec((1, 3, d), lambda i, n: (i // tiles_per_seq, 0, 0)),
                  pl.BlockSpec((1, d), lambda i, n: (0, 0)),
                  pl.BlockSpec((d, bn), lambda i, n: (0, n)),
                  pl.BlockSpec((d, bn), lambda i, n: (0, n + nb)),
                  pl.BlockSpec((1, bn), lambda i, n: (0, n)),
                  pl.BlockSpec((1, bn), lambda i, n: (0, n + nb))],
        out_specs=pl.BlockSpec((bm, bn), lambda i, n: (i, n)),
        scratch_shapes=[pltpu.VMEM((bm, d), BF16)],
        compiler_params=_params(2),
        name="in_proj_glu",
    )(x, mod, g, w, w, bias2, bias2)


def _in_ffn_kernel(x_ref, mod_ref, g_ref, wg_ref, wu_ref, wdw_ref, o_ref, h_ref, carry_ref, *,
                   tiles_per_seq):
    i = pl.program_id(0)
    n = pl.program_id(1)

    @pl.when(n == 0)
    def _():
        h_ref[...] = _modulated_norm(x_ref[...], mod_ref, g_ref).astype(BF16)

    h = h_ref[...]
    gate = jnp.dot(h, wg_ref[...], preferred_element_type=F32)
    up = jnp.dot(h, wu_ref[...], preferred_element_type=F32)
    bm = gate.shape[0]

    @pl.when((i % tiles_per_seq) == 0)
    def _():
        carry_ref[n] = jnp.zeros(carry_ref.shape[1:], F32)

    prev = carry_ref[n]
    row = lax.broadcasted_iota(jnp.int32, gate.shape, 0)
    g1 = jnp.where(row == 0, prev[7:8, :], pltpu.roll(gate, 1, axis=0))
    g2 = pltpu.roll(gate, 2, axis=0)
    g2 = jnp.where(row == 0, prev[6:7, :], jnp.where(row == 1, prev[7:8, :], g2))
    carry_ref[n] = gate[bm - 8:, :]

    conv = wdw_ref[0:1, :] * g2 + wdw_ref[1:2, :] * g1 + wdw_ref[2:3, :] * gate
    o_ref[...] = (_silu(conv) * up).astype(o_ref.dtype)


def _in_proj_ffn(x, mod, g, w, w_dw, *, seq):
    m, d = x.shape
    f = w.shape[1] // 2
    bm = _pick(seq, 512)
    bn = _pick(f, 512)
    nb = f // bn
    tiles_per_seq = seq // bm
    kern = functools.partial(_in_ffn_kernel, tiles_per_seq=tiles_per_seq)
    return pl.pallas_call(
        kern,
        out_shape=jax.ShapeDtypeStruct((m, f), BF16),
        grid=(m // bm, nb),
        in_specs=[pl.BlockSpec((bm, d), lambda i, n: (i, 0)),
                  pl.BlockSpec((1, 3, d), lambda i, n: (i // tiles_per_seq, 0, 0)),
                  pl.BlockSpec((1, d), lambda i, n: (0, 0)),
                  pl.BlockSpec((d, bn), lambda i, n: (0, n)),
                  pl.BlockSpec((d, bn), lambda i, n: (0, n + nb)),
                  pl.BlockSpec((3, bn), lambda i, n: (0, n))],
        out_specs=pl.BlockSpec((bm, bn), lambda i, n: (i, n)),
        scratch_shapes=[pltpu.VMEM((bm, d), BF16), pltpu.VMEM((nb, 8, bn), F32)],
        compiler_params=_params(2),
        name="in_proj_ffn",
    )(x, mod, g, w, w, w_dw)


def _out_kernel(*refs, nk, has_bias):
    if has_bias:
        a_ref, w_ref, bias_ref, x_ref, mod_ref, g_ref, o_ref = refs[:7]
    else:
        a_ref, w_ref, x_ref, mod_ref, g_ref, o_ref = refs[:6]
    acc_ref = refs[-1] if nk > 1 else None
    k = pl.program_id(1)
    part = jnp.dot(a_ref[...], w_ref[...], preferred_element_type=F32)

    def finish(y):
        if has_bias:
            y = y + bias_ref[...]
        ms = jnp.mean(y * y, axis=-1, keepdims=True)
        yn = y * lax.rsqrt(ms + EPS) * g_ref[...]
        o_ref[...] = x_ref[...] + mod_ref[0, 2:3, :] * yn

    if nk == 1:
        finish(part)
        return

    @pl.when(k == 0)
    def _():
        acc_ref[...] = part

    @pl.when((k > 0) & (k < nk - 1))
    def _():
        acc_ref[...] += part

    @pl.when(k == nk - 1)
    def _():
        finish(acc_ref[...] + part)


def _out_proj(a, w, bias, x, mod, g, *, seq):
    m, kdim = a.shape
    d = w.shape[1]
    bm = _pick(seq, 512)
    bk = kdim if kdim <= d else _pick(kdim, 1536)
    nk = kdim // bk
    tiles_per_seq = seq // bm
    has_bias = bias is not None
    kern = functools.partial(_out_kernel, nk=nk, has_bias=has_bias)
    in_specs = [pl.BlockSpec((bm, bk), lambda i, k: (i, k)),
                pl.BlockSpec((bk, d), lambda i, k: (k, 0))]
    args = [a, w]
    if has_bias:
        in_specs.append(pl.BlockSpec((1, d), lambda i, k: (0, 0)))
        args.append(bias.reshape(1, d))
    in_specs += [pl.BlockSpec((bm, d), lambda i, k: (i, 0)),
                 pl.BlockSpec((1, 3, d), lambda i, k: (i // tiles_per_seq, 0, 0)),
                 pl.BlockSpec((1, d), lambda i, k: (0, 0))]
    args += [x, mod, g]
    return pl.pallas_call(
        kern,
        out_shape=jax.ShapeDtypeStruct((m, d), F32),
        grid=(m // bm, nk),
        in_specs=in_specs,
        out_specs=pl.BlockSpec((bm, d), lambda i, k: (i, 0)),
        scratch_shapes=[pltpu.VMEM((bm, d), F32)] if nk > 1 else [],
        compiler_params=_params(2),
        name="out_proj",
    )(*args)


def _ret_kernel(lg_ref, q_ref, k_ref, v_ref, g_ref, o_ref, state_ref, *, chunk, n_chunks):
    @pl.when(pl.program_id(2) == 0)
    def _():
        state_ref[...] = jnp.zeros_like(state_ref)

    lg = lg_ref[0, 0:1, 0:1]
    ii = lax.broadcasted_iota(jnp.int32, (chunk, chunk), 0)
    jj = lax.broadcasted_iota(jnp.int32, (chunk, chunk), 1)
    rel = (ii - jj).astype(F32)
    decay = jnp.where(rel >= 0, jnp.exp(jnp.maximum(rel, 0.0) * lg), 0.0)
    idx = lax.broadcasted_iota(jnp.int32, (chunk, 1), 0).astype(F32)
    cross_decay = jnp.exp((idx + 1.0) * lg)
    state_decay = jnp.exp((chunk - 1.0 - idx) * lg)
    chunk_decay = jnp.exp(chunk * lg)

    for c in range(n_chunks):
        rows = pl.ds(c * chunk, chunk)
        qc = q_ref[rows, :]
        kc = k_ref[rows, :]
        vc = v_ref[rows, :]
        scores = lax.dot_general(qc, kc, (((1,), (1,)), ((), ())),
                                 preferred_element_type=F32) * decay
        inner = jnp.dot(scores.astype(BF16), vc, preferred_element_type=F32)
        state = state_ref[...]
        cross = jnp.dot(qc, state.astype(BF16), preferred_element_type=F32) * cross_decay
        kd = (kc.astype(F32) * state_decay).astype(BF16)
        upd = lax.dot_general(kd, vc, (((0,), (0,)), ((), ())), preferred_element_type=F32)
        state_ref[...] = state * chunk_decay + upd
        o = inner + cross
        mu = jnp.mean(o, axis=-1, keepdims=True)
        var = jnp.mean(jnp.square(o - mu), axis=-1, keepdims=True)
        on = (o - mu) * lax.rsqrt(var + EPS)
        gate = g_ref[rows, :].astype(F32)
        o_ref[rows, :] = (_silu(gate) * on).astype(o_ref.dtype)


def _retention_core(qkvg, log_gamma, *, batch, seq, heads):
    m = qkvg.shape[0]
    chunk = min(256, seq)
    ts = _pick(seq, 1024)
    n_chunks = ts // chunk
    nsb = seq // ts
    kern = functools.partial(_ret_kernel, chunk=chunk, n_chunks=n_chunks)
    lg = jnp.broadcast_to(log_gamma.astype(F32)[:, None, None], (heads, 8, LANES))
    k_off = heads
    v_off = 2 * heads * RET_DK // RET_DV
    g_off = v_off + heads
    return pl.pallas_call(
        kern,
        out_shape=jax.ShapeDtypeStruct((m, heads * RET_DV), BF16),
        grid=(batch, heads, nsb),
        in_specs=[pl.BlockSpec((1, 8, LANES), lambda b, h, s: (h, 0, 0)),
                  pl.BlockSpec((ts, RET_DK), lambda b, h, s: (b * nsb + s, h)),
                  pl.BlockSpec((ts, RET_DK), lambda b, h, s: (b * nsb + s, k_off + h)),
                  pl.BlockSpec((ts, RET_DV), lambda b, h, s: (b * nsb + s, v_off + h)),
                  pl.BlockSpec((ts, RET_DV), lambda b, h, s: (b * nsb + s, g_off + h))],
        out_specs=pl.BlockSpec((ts, RET_DV), lambda b, h, s: (b * nsb + s, h)),
        scratch_shapes=[pltpu.VMEM((RET_DK, RET_DV), F32)],
        compiler_params=_params(3),
        name="retention_core",
    )(lg, qkvg, qkvg, qkvg, qkvg)


def _dwconv_kernel(u_ref, halo_ref, w_ref, bdw_ref, lng_ref, lnb_ref, o_ref, buf_ref, y_ref, *,
                   width):
    ts, d = u_ref.shape
    buf_ref[CONV_HALO:, :] = u_ref[...]

    @pl.when(pl.program_id(1) == 0)
    def _():
        buf_ref[0:CONV_HALO, :] = jnp.zeros((CONV_HALO, d), F32)

    @pl.when(pl.program_id(1) > 0)
    def _():
        buf_ref[0:CONV_HALO, :] = halo_ref[...]

    base = CONV_HALO - (width - 1)

    def lane_chunk(cb, carry):
        cols = pl.ds(pl.multiple_of(cb * LANES, LANES), LANES)
        acc = jnp.zeros((ts, LANES), F32) + bdw_ref[:, cols]
        for j in range(width):
            acc = acc + w_ref[j:j + 1, cols] * buf_ref[base + j:base + j + ts, cols]
        y_ref[:, cols] = acc
        return carry

    lax.fori_loop(0, d // LANES, lane_chunk, 0)

    y = y_ref[...]
    mu = jnp.mean(y, axis=-1, keepdims=True)
    var = jnp.mean(jnp.square(y - mu), axis=-1, keepdims=True)
    un = (y - mu) * lax.rsqrt(var + EPS) * lng_ref[...] + lnb_ref[...]
    o_ref[...] = _silu(un).astype(o_ref.dtype)


def _conformer_core(u, w_dw, b_dw, ln_g, ln_b, *, batch, seq):
    m, d = u.shape
    width = w_dw.shape[0]
    assert width - 1 <= CONV_HALO
    ts = _pick(seq, 256)
    nt = seq // ts
    halo_per_tile = ts // CONV_HALO
    kern = functools.partial(_dwconv_kernel, width=width)
    row = lambda v: v.reshape(1, d)
    return pl.pallas_call(
        kern,
        out_shape=jax.ShapeDtypeStruct((m, d), BF16),
        grid=(batch, nt),
        in_specs=[pl.BlockSpec((ts, d), lambda b, t: (b * nt + t, 0)),
                  pl.BlockSpec((CONV_HALO, d),
                               lambda b, t: (jnp.maximum((b * nt + t) * halo_per_tile - 1, 0), 0)),
                  pl.BlockSpec((width, d), lambda b, t: (0, 0)),
                  pl.BlockSpec((1, d), lambda b, t: (0, 0)),
                  pl.BlockSpec((1, d), lambda b, t: (0, 0)),
                  pl.BlockSpec((1, d), lambda b, t: (0, 0))],
        out_specs=pl.BlockSpec((ts, d), lambda b, t: (b * nt + t, 0)),
        scratch_shapes=[pltpu.VMEM((ts + CONV_HALO, d), F32), pltpu.VMEM((ts, d), F32)],
        compiler_params=_params(2),
        name="conformer_core",
    )(u, u, w_dw, row(b_dw), row(ln_g), row(ln_b))


def _diff_attn_kernel(lam_ref, sg_ref, q_ref, k_ref, v_ref, o_ref, acc_ref, *, tq, lambda_init):
    qi = pl.program_id(2)
    dh = DIFF_DH

    def scores(c, rows):
        return lax.dot_general(q_ref[:, c * dh:(c + 1) * dh], k_ref[rows, c * dh:(c + 1) * dh],
                               (((1,), (1,)), ((), ())), preferred_element_type=F32)

    def update(c, s, m, l, vb):
        m_new = jnp.maximum(m, jnp.max(s, axis=-1, keepdims=True))
        alpha = jnp.exp(m - m_new)
        p = jnp.exp(s - m_new)
        l_new = alpha * l + jnp.sum(p, axis=-1, keepdims=True)
        acc_ref[c] = alpha * acc_ref[c] + jnp.dot(p.astype(BF16), vb, preferred_element_type=F32)
        return m_new, l_new

    acc_ref[...] = jnp.zeros_like(acc_ref)
    init = (jnp.full((tq, 1), NEG_INF, F32), jnp.zeros((tq, 1), F32))

    def body(j, carry):
        rows = pl.ds(pl.multiple_of(j * tq, tq), tq)
        vb = v_ref[rows, :]
        return tuple(update(c, scores(c, rows), *carry[c], vb) for c in range(2))

    carry = lax.fori_loop(0, qi, body, (init, init))

    rows = pl.ds(pl.multiple_of(qi * tq, tq), tq)
    vb = v_ref[rows, :]
    qpos = lax.broadcasted_iota(jnp.int32, (tq, tq), 0)
    kpos = lax.broadcasted_iota(jnp.int32, (tq, tq), 1)
    outs = []
    for c in range(2):
        s = jnp.where(kpos <= qpos, scores(c, rows), NEG_INF)
        _, l = update(c, s, *carry[c], vb)
        outs.append(acc_ref[c] / l)

    lp = lam_ref[...]
    lam = (jnp.exp(jnp.sum(lp[0:1] * lp[1:2], axis=-1, keepdims=True))
           - jnp.exp(jnp.sum(lp[2:3] * lp[3:4], axis=-1, keepdims=True)) + lambda_init)
    o = outs[0] - lam * outs[1]
    ms = jnp.mean(o * o, axis=-1, keepdims=True)
    o_ref[...] = (o * lax.rsqrt(ms + EPS) * sg_ref[...] * (1.0 - lambda_init)).astype(o_ref.dtype)


def _diff_attn_core(qkv, lam_params, subln_g, *, batch, seq, heads, lambda_init):
    m = qkv.shape[0]
    hw = 2 * DIFF_DH
    tq = _pick(seq, 256)
    nq = seq // tq
    kern = functools.partial(_diff_attn_kernel, tq=tq, lambda_init=lambda_init)
    return pl.pallas_call(
        kern,
        out_shape=jax.ShapeDtypeStruct((m, heads * hw), BF16),
        grid=(batch, heads, nq),
        in_specs=[pl.BlockSpec((4, DIFF_DH), lambda b, h, q: (0, 0)),
                  pl.BlockSpec((1, hw), lambda b, h, q: (0, 0)),
                  pl.BlockSpec((tq, hw), lambda b, h, q: (b * nq + q, h)),
                  pl.BlockSpec((seq, hw), lambda b, h, q: (b, heads + h)),
                  pl.BlockSpec((seq, hw), lambda b, h, q: (b, 2 * heads + h))],
        out_specs=pl.BlockSpec((tq, hw), lambda b, h, q: (b * nq + q, h)),
        scratch_shapes=[pltpu.VMEM((2, tq, hw), F32)],
        compiler_params=_params(3),
        name="diff_attn_core",
    )(lam_params.astype(F32), subln_g.astype(F32).reshape(1, hw), qkv, qkv, qkv)


def _rope_angles(positions, dim):
    inv_freq = ROPE_THETA ** (-jnp.arange(0, dim, 2, dtype=F32) / dim)
    ang = positions.astype(F32)[..., None] * inv_freq
    return ang.reshape(-1, dim // 2)


def kernel(x, c, positions, mod_w, mod_b, norm_g, ret_w_in, ret_w_out, conv_w_pw1, conv_b_pw1, conv_w_dw, conv_b_dw, conv_ln_g, conv_ln_b, conv_w_pw2, conv_b_pw2, diff_w_in, diff_lambda, diff_subln_g, diff_w_out, ffn_w_in, ffn_w_dw, ffn_w_down):
    batch, seq, d = x.shape
    depth = mod_w.shape[0]
    n_mixers = 3
    ret_heads = d // RET_DK
    diff_heads = d // (2 * DIFF_DH)

    mods = _modulation(c, mod_w, mod_b).reshape(2 * depth, batch, 3, d)
    xf = x.reshape(batch * seq, d)

    ang = _rope_angles(positions, RET_DK)
    ret_cos, ret_sin = jnp.cos(ang), jnp.sin(ang)
    ang = _rope_angles(positions, DIFF_DH)
    diff_t1 = jnp.concatenate([jnp.cos(ang), jnp.cos(ang)], axis=-1)
    diff_t2 = jnp.concatenate([-jnp.sin(ang), jnp.sin(ang)], axis=-1)
    log_gamma = jnp.log1p(-jnp.exp2(-5.0 - jnp.arange(ret_heads, dtype=F32)))

    for i in range(depth):
        kind, slot = i % n_mixers, i // n_mixers
        mod = mods[2 * i]
        g_pre = norm_g[i, 0].reshape(1, d)
        g_post = norm_g[i, 1].reshape(1, d)
        if kind == 0:
            qk_cols = 2 * ret_heads * RET_DK
            col_scale = jnp.concatenate([jnp.ones((1, qk_cols // 2), F32),
                                         jnp.full((1, qk_cols // 2), RET_DK ** -0.5, F32)], axis=1)
            qkvg = _in_proj_rot(xf, mod, g_pre, ret_w_in[slot].astype(BF16), col_scale, ret_cos, ret_sin,
                                seq=seq, rot_cols=qk_cols, pair_width=RET_DK)
            a = _retention_core(qkvg, log_gamma, batch=batch, seq=seq, heads=ret_heads)
            xf = _out_proj(a, ret_w_out[slot].astype(BF16), None, xf, mod, g_post, seq=seq)
        elif kind == 1:
            u = _in_proj_glu(xf, mod, g_pre, conv_w_pw1[slot].astype(BF16), conv_b_pw1[slot], seq=seq)
            a = _conformer_core(u, conv_w_dw[slot], conv_b_dw[slot], conv_ln_g[slot], conv_ln_b[slot],
                                batch=batch, seq=seq)
            xf = _out_proj(a, conv_w_pw2[slot].astype(BF16), conv_b_pw2[slot], xf, mod, g_post, seq=seq)
        else:
            qk_cols = 2 * d
            col_scale = jnp.concatenate([jnp.full((1, d), DIFF_DH ** -0.5, F32),
                                         jnp.ones((1, d), F32)], axis=1)
            qkv = _in_proj_rot(xf, mod, g_pre, diff_w_in[slot].astype(BF16), col_scale, diff_t1, diff_t2,
                               seq=seq, rot_cols=qk_cols, pair_width=DIFF_DH)
            lambda_init = 0.8 - 0.6 * math.exp(-0.3 * i)
            a = _diff_attn_core(qkv, diff_lambda[slot], diff_subln_g[slot], batch=batch, seq=seq,
                                heads=diff_heads, lambda_init=lambda_init)
            xf = _out_proj(a, diff_w_out[slot].astype(BF16), None, xf, mod, g_post, seq=seq)

        mod = mods[2 * i + 1]
        u = _in_proj_ffn(xf, mod, norm_g[i, 2].reshape(1, d), ffn_w_in[i].astype(BF16), ffn_w_dw[i], seq=seq)
        xf = _out_proj(u, ffn_w_down[i].astype(BF16), None, xf, mod, norm_g[i, 3].reshape(1, d), seq=seq)
    return xf.reshape(batch, seq, d)
```
